```python
import math
import jax, jax.numpy as jnp
from jax import lax
import numpy as np

D_MODEL = 2048
BATCH = 4
SEQ = 2048
DEPTH = 2
DEC_BATCH = 8
DEC_SEQ = 8
PAST_LEN = 16384
PAGE_SIZE = 128

MIX_WIDTH = D_MODEL
ATTN_WIDTH = MIX_WIDTH // 2
N_ATTN_HEADS = 8
DK = 64
QK_HEAD_DIM = 2 * DK
V_HEAD_DIM = ATTN_WIDTH // N_ATTN_HEADS
D_INNER = MIX_WIDTH - ATTN_WIDTH
SSM_HEAD_DIM = 64
SSM_HEADS = D_INNER // SSM_HEAD_DIM
SSM_GROUPS = 2
HEADS_PER_GROUP = SSM_HEADS // SSM_GROUPS
D_STATE = 128
CONV_W = 4
CONV_DIM = D_INNER + 2 * SSM_GROUPS * D_STATE
SSD_CHUNK = 128
Q_BLOCK = 128
O_Q = 0
O_K = O_Q + N_ATTN_HEADS * QK_HEAD_DIM
O_V = O_K + N_ATTN_HEADS * QK_HEAD_DIM
O_Z = O_V + ATTN_WIDTH
O_XBC = O_Z + D_INNER
O_DT = O_XBC + CONV_DIM
IN_DIM = O_DT + SSM_HEADS
D_FF = 5632
N_EXPERTS = 8
TOP_K = 2
D_EXPERT = D_FF // 2
N_DENSE = (DEPTH + 1) // 2
N_MOE = DEPTH // 2
EPS = 1e-6

kernel_name = 'hybrid_diffattn_ssd_decode_step'


def rms_norm(x, gain):
    xf = x.astype(jnp.float32)
    y = xf * lax.rsqrt(jnp.mean(xf * xf, axis=-1, keepdims=True) + EPS)
    return (y * gain.astype(jnp.float32)).astype(x.dtype)


def alibi_slopes():
    return 2.0 ** (-8.0 * jnp.arange(1, N_ATTN_HEADS + 1, dtype=jnp.float32) / N_ATTN_HEADS)


def lambda_init(layer):
    return 0.8 - 0.6 * math.exp(-0.3 * layer)


def diff_attention(q, k, v, q_pos, k_pos, lam):
    b, lq = q.shape[0], q.shape[1]
    blk = Q_BLOCK if lq % Q_BLOCK == 0 else lq
    nb = lq // blk
    qb = jnp.moveaxis(q.reshape(b, nb, blk, N_ATTN_HEADS, 2, DK), 1, 0)
    pb = q_pos.reshape(nb, blk)
    slopes = alibi_slopes()

    def one_block(args):
        qi, pi = args
        s = jnp.einsum('bqhcd,bkhcd->bchqk', qi, k).astype(jnp.float32)
        dist = pi[:, None] - k_pos[None, :]
        bias = -slopes[:, None, None] * dist.astype(jnp.float32)[None]
        s = jnp.where(dist >= 0, s + bias, -jnp.inf)
        p = jax.nn.softmax(s, axis=-1)
        a = p[:, 0] - lam * p[:, 1]
        return jnp.einsum('bhqk,bkhd->bqhd', a.astype(v.dtype), v)

    out = lax.map(one_block, (qb, pb))
    return jnp.moveaxis(out, 0, 1).reshape(b, lq, N_ATTN_HEADS, V_HEAD_DIM)


def causal_conv(x_ext, w, bias):
    y = lax.conv_general_dilated(x_ext, w[:, None, :].astype(x_ext.dtype), (1,), 'VALID',
                                 dimension_numbers=('NWC', 'WIO', 'NWC'),
                                 feature_group_count=CONV_DIM)
    return jax.nn.silu(y + bias.astype(x_ext.dtype))


def ssd(x, dt, A, Bm, Cm, h0):
    b, L, G, R, P = x.shape
    N = Bm.shape[-1]
    lc = SSD_CHUNK if L % SSD_CHUNK == 0 else L
    nc = L // lc
    x = x.astype(jnp.float32).reshape(b, nc, lc, G, R, P)
    dt = dt.astype(jnp.float32).reshape(b, nc, lc, G, R)
    Bm = Bm.astype(jnp.float32).reshape(b, nc, lc, G, N)
    Cm = Cm.astype(jnp.float32).reshape(b, nc, lc, G, N)
    acs = jnp.cumsum(dt * A, axis=2)
    causal = jnp.tril(jnp.ones((lc, lc), bool))[:, :, None, None]
    seg = acs[:, :, :, None] - acs[:, :, None, :]
    decay = jnp.exp(jnp.where(causal, seg, -jnp.inf))
    cb = jnp.einsum('bclgn,bcsgn->bclsg', Cm, Bm)
    w = cb[..., None] * decay * dt[:, :, None]
    y_diag = jnp.einsum('bclsgr,bcsgrp->bclgrp', w, x)
    decay_end = jnp.exp(acs[:, :, -1:] - acs) * dt
    states = jnp.einsum('bclgn,bclgr,bclgrp->bcgrpn', Bm, decay_end, x)
    chunk_decay = jnp.exp(acs[:, :, -1])

    def step(h, inp):
        s_c, d_c = inp
        return h * d_c[..., None, None] + s_c, h

    h_last, h_in = lax.scan(step, h0.astype(jnp.float32),
                            (jnp.moveaxis(states, 1, 0), jnp.moveaxis(chunk_decay, 1, 0)))
    h_in = jnp.moveaxis(h_in, 0, 1)
    y_off = jnp.einsum('bclgn,bcgrpn->bclgrp', Cm, h_in) * jnp.exp(acs)[..., None]
    return (y_diag + y_off).reshape(b, L, G, R, P), h_last


def hybrid_mixer(h, layer, lam, w_in, q_gain, k_gain, subln_gain, conv_w, conv_b, dt_bias,
                 a_log, d_skip, ssm_gain, w_out, past_k, past_v, conv_buf, ssm_h0):
    b, L, _ = h.shape
    u = h @ w_in
    q = u[..., O_Q:O_K].reshape(b, L, N_ATTN_HEADS, 2, DK)
    k = u[..., O_K:O_V].reshape(b, L, N_ATTN_HEADS, 2, DK)
    v = u[..., O_V:O_Z].reshape(b, L, N_ATTN_HEADS, V_HEAD_DIM)
    z = u[..., O_Z:O_XBC]
    xbc = u[..., O_XBC:O_DT]
    dt_raw = u[..., O_DT:IN_DIM]
    q = rms_norm(q, q_gain) * (DK ** -0.5)
    k = rms_norm(k, k_gain)
    if past_k is None:
        past = 0
        keys, vals = k, v
    else:
        past = past_k.shape[1]
        keys = jnp.concatenate([past_k, k], axis=1)
        vals = jnp.concatenate([past_v, v], axis=1)
    q_pos = past + jnp.arange(L, dtype=jnp.int32)
    k_pos = jnp.arange(past + L, dtype=jnp.int32)
    o = diff_attention(q, keys, vals, q_pos, k_pos, lam)
    o = rms_norm(o, subln_gain) * (1.0 - lambda_init(layer))
    x_ext = jnp.concatenate([conv_buf.astype(xbc.dtype), xbc], axis=1)
    new_conv = x_ext[:, x_ext.shape[1] - (CONV_W - 1):]
    xbc_c = causal_conv(x_ext, conv_w, conv_b)
    xs = xbc_c[..., :D_INNER].reshape(b, L, SSM_GROUPS, HEADS_PER_GROUP, SSM_HEAD_DIM)
    Bm = xbc_c[..., D_INNER:D_INNER + SSM_GROUPS * D_STATE].reshape(b, L, SSM_GROUPS, D_STATE)
    Cm = xbc_c[..., D_INNER + SSM_GROUPS * D_STATE:].reshape(b, L, SSM_GROUPS, D_STATE)
    dt = jax.nn.softplus(dt_raw.astype(jnp.float32) + dt_bias.astype(jnp.float32))
    dt = dt.reshape(b, L, SSM_GROUPS, HEADS_PER_GROUP)
    A = -jnp.exp(a_log.astype(jnp.float32)).reshape(SSM_GROUPS, HEADS_PER_GROUP)
    h0 = ssm_h0.reshape(b, SSM_GROUPS, HEADS_PER_GROUP, SSM_HEAD_DIM, D_STATE)
    y, h_last = ssd(xs, dt, A, Bm, Cm, h0)
    y = y + d_skip.astype(jnp.float32).reshape(SSM_GROUPS, HEADS_PER_GROUP)[..., None] * xs.astype(jnp.float32)
    y = y.reshape(b, L, D_INNER).astype(h.dtype) * jax.nn.silu(z)
    y = rms_norm(y.reshape(b, L, SSM_GROUPS, D_INNER // SSM_GROUPS),
                 ssm_gain.reshape(SSM_GROUPS, D_INNER // SSM_GROUPS)).reshape(b, L, D_INNER)
    out = jnp.concatenate([o.reshape(b, L, ATTN_WIDTH), y], axis=-1) @ w_out
    return (out, k.reshape(b, L, N_ATTN_HEADS, QK_HEAD_DIM), v, new_conv,
            h_last.reshape(b, SSM_HEADS, SSM_HEAD_DIM, D_STATE))


def swiglu(h, wg, wu, wd):
    return (jax.nn.silu(h @ wg) * (h @ wu)) @ wd


def moe_swiglu(h, router_w, wg, wu, wd):
    logits = (h @ router_w).astype(jnp.float32)
    probs = jax.nn.softmax(logits, axis=-1)
    vals, idx = lax.top_k(probs, TOP_K)
    vals = vals / jnp.sum(vals, axis=-1, keepdims=True)
    gates = jnp.sum(jax.nn.one_hot(idx, N_EXPERTS, dtype=jnp.float32) * vals[..., None], axis=-2)
    y = jnp.zeros_like(h)
    for e in range(N_EXPERTS):
        y = y + gates[..., e:e + 1].astype(h.dtype) * swiglu(h, wg[e], wu[e], wd[e])
    return y


def setup_inputs(seed: int = 0) -> dict:
    key = jax.random.key(seed)
    ks = jax.random.split(key, 32)
    f32 = jnp.float32
    n_pages = PAST_LEN // PAGE_SIZE
    n_used = DEC_BATCH * n_pages
    n_pool = (n_used * 5 + 3) // 4

    def nrm(k, shape, scale):
        return scale * jax.random.normal(k, shape, f32)

    def gain(k, shape):
        return 1.0 + 0.02 * jax.random.normal(k, shape, f32)

    dt0 = jnp.exp(jax.random.uniform(ks[15], (DEPTH, SSM_HEADS), f32, math.log(1e-3), math.log(1e-1)))
    return {
        'x_prompt': nrm(ks[0], (BATCH, SEQ, D_MODEL), 1.0),
        'x_sample': nrm(ks[1], (DEC_BATCH, DEC_SEQ, D_MODEL), 1.0),
        'cache_k': nrm(ks[2], (DEPTH, n_pool, PAGE_SIZE, N_ATTN_HEADS, QK_HEAD_DIM), 1.0),
        'cache_v': nrm(ks[3], (DEPTH, n_pool, PAGE_SIZE, N_ATTN_HEADS, V_HEAD_DIM), 1.0),
        'page_table': jax.random.permutation(ks[4], n_pool)[:n_used].reshape(DEC_BATCH, n_pages).astype(jnp.int32),
        'state_conv': nrm(ks[5], (DEPTH, DEC_BATCH, CONV_W - 1, CONV_DIM), 1.0),
        'state_ssm': nrm(ks[6], (DEPTH, DEC_BATCH, SSM_HEADS, SSM_HEAD_DIM, D_STATE), 0.1),
        'norm_mix': gain(ks[7], (DEPTH, D_MODEL)),
        'w_in': nrm(ks[8], (DEPTH, D_MODEL, IN_DIM), D_MODEL ** -0.5),
        'q_norm': gain(ks[9], (DEPTH, DK)),
        'k_norm': gain(ks[10], (DEPTH, DK)),
        'lambda_q1': nrm(ks[11], (DEPTH, DK), 0.1),
        'lambda_k1': nrm(ks[12], (DEPTH, DK), 0.1),
        'lambda_q2': nrm(ks[13], (DEPTH, DK), 0.1),
        'lambda_k2': nrm(ks[14], (DEPTH, DK), 0.1),
        'attn_subln': gain(ks[16], (DEPTH, V_HEAD_DIM)),
        'conv_w': nrm(ks[17], (DEPTH, CONV_W, CONV_DIM), CONV_W ** -0.5),
        'conv_b': nrm(ks[18], (DEPTH, CONV_DIM), 0.02),
        'dt_bias': dt0 + jnp.log(-jnp.expm1(-dt0)),
        'a_log': jnp.log(jax.random.uniform(ks[19], (DEPTH, SSM_HEADS), f32, 1.0, 16.0)),
        'd_skip': 1.0 + 0.1 * jax.random.normal(ks[20], (DEPTH, SSM_HEADS), f32),
        'ssm_norm': gain(ks[21], (DEPTH, D_INNER)),
        'w_out': nrm(ks[22], (DEPTH, MIX_WIDTH, D_MODEL), MIX_WIDTH ** -0.5),
        'norm_ffn': gain(ks[23], (DEPTH, D_MODEL)),
        'dense_w_gate': nrm(ks[24], (N_DENSE, D_MODEL, D_FF), D_MODEL ** -0.5),
        'dense_w_up': nrm(ks[25], (N_DENSE, D_MODEL, D_FF), D_MODEL ** -0.5),
        'dense_w_down': nrm(ks[26], (N_DENSE, D_FF, D_MODEL), D_FF ** -0.5),
        'router_w': nrm(ks[27], (N_MOE, D_MODEL, N_EXPERTS), D_MODEL ** -0.5),
        'expert_w_gate': nrm(ks[28], (N_MOE, N_EXPERTS, D_MODEL, D_EXPERT), D_MODEL ** -0.5),
        'expert_w_up': nrm(ks[29], (N_MOE, N_EXPERTS, D_MODEL, D_EXPERT), D_MODEL ** -0.5),
        'expert_w_down': nrm(ks[30], (N_MOE, N_EXPERTS, D_EXPERT, D_MODEL), D_EXPERT ** -0.5),
    }


def reference(x_prompt, x_sample, cache_k, cache_v, page_table, state_conv, state_ssm,
              norm_mix, w_in, q_norm, k_norm, lambda_q1, lambda_k1, lambda_q2, lambda_k2,
              attn_subln, conv_w, conv_b, dt_bias, a_log, d_skip, ssm_norm, w_out, norm_ffn,
              dense_w_gate, dense_w_up, dense_w_down, router_w, expert_w_gate, expert_w_up,
              expert_w_down):
    db, n_pages = page_table.shape
    past_len = n_pages * cache_k.shape[2]
    bp = x_prompt.shape[0]
    xp, xs = x_prompt, x_sample
    kp_l, vp_l, cp_l, sp_l, ks_l, vs_l, cs_l, ss_l = [], [], [], [], [], [], [], []
    for l in range(DEPTH):
        f32 = jnp.float32
        lam = (jnp.exp(jnp.sum(lambda_q1[l].astype(f32) * lambda_k1[l].astype(f32)))
               - jnp.exp(jnp.sum(lambda_q2[l].astype(f32) * lambda_k2[l].astype(f32)))
               + lambda_init(l))
        lw = (w_in[l], q_norm[l], k_norm[l], attn_subln[l], conv_w[l], conv_b[l], dt_bias[l],
              a_log[l], d_skip[l], ssm_norm[l], w_out[l])
        hp = rms_norm(xp, norm_mix[l])
        mp, kp, vp, cp, sp = hybrid_mixer(
            hp, l, lam, *lw, past_k=None, past_v=None,
            conv_buf=jnp.zeros((bp, CONV_W - 1, CONV_DIM), hp.dtype),
            ssm_h0=jnp.zeros((bp, SSM_HEADS, SSM_HEAD_DIM, D_STATE), jnp.float32))
        past_k = cache_k[l][page_table].reshape(db, past_len, N_ATTN_HEADS, 2, DK)
        past_v = cache_v[l][page_table].reshape(db, past_len, N_ATTN_HEADS, V_HEAD_DIM)
        hs = rms_norm(xs, norm_mix[l])
        ms, ks_, vs_, cs_, ss_ = hybrid_mixer(
            hs, l, lam, *lw, past_k=past_k, past_v=past_v,
            conv_buf=state_conv[l], ssm_h0=state_ssm[l])
        xp = xp + mp
        xs = xs + ms
        hp = rms_norm(xp, norm_ffn[l])
        hs = rms_norm(xs, norm_ffn[l])
        j = l // 2
        if l % 2 == 0:
            xp = xp + swiglu(hp, dense_w_gate[j], dense_w_up[j], dense_w_down[j])
            xs = xs + swiglu(hs, dense_w_gate[j], dense_w_up[j], dense_w_down[j])
        else:
            xp = xp + moe_swiglu(hp, router_w[j], expert_w_gate[j], expert_w_up[j], expert_w_down[j])
            xs = xs + moe_swiglu(hs, router_w[j], expert_w_gate[j], expert_w_up[j], expert_w_down[j])
        kp_l.append(kp); vp_l.append(vp); cp_l.append(cp); sp_l.append(sp)
        ks_l.append(ks_); vs_l.append(vs_); cs_l.append(cs_); ss_l.append(ss_)
    new_k_prompt = jnp.stack(kp_l)
    new_v_prompt = jnp.stack(vp_l)
    new_conv_prompt = jnp.stack(cp_l)
    new_ssm_prompt = jnp.stack(sp_l)
    new_k_sample = jnp.stack(ks_l)
    new_v_sample = jnp.stack(vs_l)
    new_conv_sample = jnp.stack(cs_l)
    new_ssm_sample = jnp.stack(ss_l)
    return (xp, xs, new_k_prompt, new_v_prompt, new_conv_prompt, new_ssm_prompt,
            new_k_sample, new_v_sample, new_conv_sample, new_ssm_sample)
```

```python
import functools
import math
from typing import NamedTuple

import jax
import jax.numpy as jnp
from jax import lax
from jax.experimental import pallas as pl
from jax.experimental.pallas import tpu as pltpu

F32 = jnp.float32
BF16 = jnp.bfloat16
EPS = 1e-6
LANES = 128
SUBLANES = 8
BF16_ROWS = 16
V7X_SCOPED_VMEM_BYTES = 60000 * 1024

NT_DIMS = (((1,), (1,)), ((), ()))
TN_DIMS = (((0,), (0,)), ((), ()))


class Dims(NamedTuple):
    d: int
    bp: int
    lp: int
    bs: int
    ls: int
    mp: int
    ms: int
    m: int
    heads: int
    dk: int
    aw: int
    di: int
    sh: int
    sp: int
    sg: int
    ns: int
    cw: int
    cd: int
    in_main: int
    past: int
    page: int
    n_pages: int
    tm: int
    tn: int
    tq: int
    lc: int
    pps: int
    tf: int


def _largest_divisor(n, cap, mult):
    best = None
    for t in range(mult, min(n, cap) + 1, mult):
        if n % t == 0:
            best = t
    assert best is not None, (n, cap, mult)
    return best


def _params(sem, vmem_bytes):
    return pltpu.CompilerParams(
        dimension_semantics=sem,
        vmem_limit_bytes=int(min(vmem_bytes, V7X_SCOPED_VMEM_BYTES)))


def _split3(x):
    hi = x.astype(BF16)
    r = x - hi.astype(F32)
    mid = r.astype(BF16)
    lo = (r - mid.astype(F32)).astype(BF16)
    return hi, mid, lo


def _rms(x, gain):
    ms = jnp.mean(x * x, axis=-1, keepdims=True)
    return x * lax.rsqrt(ms + EPS) * gain


def _silu(x):
    return x * jax.nn.sigmoid(x)


def _inproj_kernel(x_ref, g_ref, w_ref, qkg_ref, bd_ref, wdh_ref, wdl_ref,
                   u_ref, dt_ref, xn_ref, *, n_qk_tiles, tn, dk):
    n = pl.program_id(1)

    @pl.when(n == 0)
    def _():
        xn = _rms(x_ref[...], g_ref[...])
        hi = xn.astype(BF16)
        lo = (xn - hi.astype(F32)).astype(BF16)
        xn_ref[...] = hi
        wh = wdh_ref[...]
        dt_ref[...] = (jnp.dot(hi, wh, preferred_element_type=F32)
                       + jnp.dot(lo, wh, preferred_element_type=F32)
                       + jnp.dot(hi, wdl_ref[...], preferred_element_type=F32))

    acc = jnp.dot(xn_ref[...], w_ref[...], preferred_element_type=F32)

    @pl.when(n < n_qk_tiles)
    def _():
        bd = bd_ref[...]
        for j in range(tn // LANES):
            sl = slice(j * LANES, (j + 1) * LANES)
            a = acc[:, sl]
            sq = a * a
            hi = sq.astype(BF16)
            lo = (sq - hi.astype(F32)).astype(BF16)
            ss = (jnp.dot(hi, bd, preferred_element_type=F32)
                  + jnp.dot(lo, bd, preferred_element_type=F32))
            u_ref[:, sl] = a * lax.rsqrt(ss * (1.0 / dk) + EPS) * qkg_ref[:, sl]

    @pl.when(n >= n_qk_tiles)
    def _():
        u_ref[...] = acc


def _inproj(dm, x, gain, w_main, qkg, bd, wdt_hi, wdt_lo):
    tm, tn, d = dm.tm, dm.tn, dm.d
    n_tiles = dm.in_main // tn
    n_qk = 2 * dm.aw // tn
    kern = functools.partial(_inproj_kernel, n_qk_tiles=n_qk, tn=tn, dk=dm.dk)
    vmem = 2 * (tm * d * 4 + d * tn * 2 + tm * tn * 4 + tm * LANES * 4 + 2 * d * LANES * 2) \
        + tm * d * 2 + 3 * tm * d * 4
    return pl.pallas_call(
        kern,
        grid=(dm.m // tm, n_tiles),
        in_specs=[
            pl.BlockSpec((tm, d), lambda m, n: (m, 0)),
            pl.BlockSpec((1, d), lambda m, n: (0, 0)),
            pl.BlockSpec((d, tn), lambda m, n: (0, n)),
            pl.BlockSpec((1, tn), lambda m, n: (0, jnp.minimum(n, n_qk - 1))),
            pl.BlockSpec((LANES, LANES), lambda m, n: (0, 0)),
            pl.BlockSpec((d, LANES), lambda m, n: (0, 0)),
            pl.BlockSpec((d, LANES), lambda m, n: (0, 0)),
        ],
        out_specs=[
            pl.BlockSpec((tm, tn), lambda m, n: (m, n)),
            pl.BlockSpec((tm, LANES), lambda m, n: (m, 0)),
        ],
        out_shape=[jax.ShapeDtypeStruct((dm.m, dm.in_main), F32),
                   jax.ShapeDtypeStruct((dm.m, LANES), F32)],
        scratch_shapes=[pltpu.VMEM((tm, d), BF16)],
        compiler_params=_params(("parallel", "arbitrary"), vmem),
        name="inproj",
    )(x, gain, w_main, qkg, bd, wdt_hi, wdt_lo)


def _online_update(sc, v_bf, m_prev, l_prev, acc_prev):
    m_new = jnp.maximum(m_prev, jnp.max(sc, axis=-1, keepdims=True))
    alpha = jnp.exp(m_prev - m_new)
    p = jnp.exp(sc - m_new)
    l_new = alpha * l_prev + jnp.sum(p, axis=-1, keepdims=True)
    acc_new = alpha * acc_prev + jnp.dot(p.astype(BF16), v_bf, preferred_element_type=F32)
    return m_new, l_new, acc_new


def _sub_ln(o, gain, post_scale):
    return _rms(o, gain) * post_scale


def _alibi_slope(h, heads):
    return 2.0 ** (-8.0 * (h + 1) / heads)


def _attn_prompt_kernel(lam_ref, q_ref, k_ref, v_ref, g_ref, o_ref,
                        qs_ref, m_ref, l_ref, acc_ref, *, tq, dk, heads, post_scale):
    h = pl.program_id(1)
    qi = pl.program_id(2)
    ki = pl.program_id(3)

    @pl.when(ki == 0)
    def _():
        q = q_ref[...]
        lane = lax.broadcasted_iota(jnp.int32, q.shape, 1)
        qs_ref[0] = jnp.where(lane < dk, q, 0.0).astype(BF16)
        qs_ref[1] = jnp.where(lane >= dk, q, 0.0).astype(BF16)
        m_ref[...] = jnp.full(m_ref.shape, -jnp.inf, F32)
        l_ref[...] = jnp.zeros(l_ref.shape, F32)
        acc_ref[...] = jnp.zeros(acc_ref.shape, F32)

    @pl.when(ki <= qi)
    def _():
        k = k_ref[...].astype(BF16)
        v = v_ref[...].astype(BF16)
        slope = jnp.exp2(-8.0 * (h + 1).astype(F32) / heads)
        row = lax.broadcasted_iota(jnp.int32, (tq, tq), 0)
        col = lax.broadcasted_iota(jnp.int32, (tq, tq), 1)
        rel = ((qi - ki) * tq + (row - col)).astype(F32)
        bias = jnp.where(rel >= 0, -slope * rel, -jnp.inf)
        for c in range(2):
            sc = lax.dot_general(qs_ref[c], k, NT_DIMS, preferred_element_type=F32) + bias
            m_new, l_new, acc_new = _online_update(sc, v, m_ref[c], l_ref[c], acc_ref[c])
            m_ref[c] = m_new
            l_ref[c] = l_new
            acc_ref[c] = acc_new

    @pl.when(ki == qi)
    def _():
        o = acc_ref[0] / l_ref[0] - lam_ref[0] * (acc_ref[1] / l_ref[1])
        o_ref[...] = _sub_ln(o, g_ref[...], post_scale)


def _attn_prompt(dm, layer_scale, lam, u, subln):
    tq, hd = dm.tq, 2 * dm.dk
    nq = dm.lp // tq
    kern = functools.partial(_attn_prompt_kernel, tq=tq, dk=dm.dk, heads=dm.heads,
                             post_scale=layer_scale)
    vmem = 2 * 4 * tq * hd * 4 + 2 * tq * hd * 2 + 2 * tq * hd * 4 + 8 * tq * tq * 4
    return pl.pallas_call(
        kern,
        grid=(dm.bp, dm.heads, nq, nq),
        in_specs=[
            pl.BlockSpec(memory_space=pltpu.SMEM),
            pl.BlockSpec((tq, hd), lambda b, h, qi, ki: (b * nq + qi, h)),
            pl.BlockSpec((tq, hd), lambda b, h, qi, ki: (b * nq + jnp.minimum(ki, qi), dm.heads + h)),
            pl.BlockSpec((tq, hd), lambda b, h, qi, ki: (b * nq + jnp.minimum(ki, qi), 2 * dm.heads + h)),
            pl.BlockSpec((1, hd), lambda b, h, qi, ki: (0, 0)),
        ],
        out_specs=pl.BlockSpec((tq, hd), lambda b, h, qi, ki: (b * nq + qi, h)),
        out_shape=jax.ShapeDtypeStruct((dm.m, dm.aw), F32),
        scratch_shapes=[pltpu.VMEM((2, tq, hd), BF16),
                        pltpu.VMEM((2, tq, 1), F32),
                        pltpu.VMEM((2, tq, 1), F32),
                        pltpu.VMEM((2, tq, hd), F32)],
        compiler_params=_params(("parallel", "parallel", "parallel", "arbitrary"), vmem),
        name="attn_prompt",
    )(lam, u, u, u, subln)


def _attn_sample_kernel(pt_ref, lam_ref, q_ref, kn_ref, vn_ref, g_ref, oprev_ref, *rest,
                        pps, heads, dk, page, past, ls, post_scale):
    del pt_ref, oprev_ref
    kp = rest[:pps]
    vp = rest[pps:2 * pps]
    o_ref, m_ref, l_ref, acc_ref = rest[2 * pps:]
    s = pl.program_id(1)
    hd = 2 * dk
    rows = 2 * ls
    tk = pps * page

    @pl.when(s == 0)
    def _():
        m_ref[...] = jnp.full(m_ref.shape, -jnp.inf, F32)
        l_ref[...] = jnp.zeros(l_ref.shape, F32)
        acc_ref[...] = jnp.zeros(acc_ref.shape, F32)

    rown = lax.broadcasted_iota(jnp.int32, (rows, hd), 0)
    lane = lax.broadcasted_iota(jnp.int32, (rows, hd), 1)
    qmask = (rown < ls) == (lane < dk)

    def q_rows(h):
        qh = q_ref[:, h * hd:(h + 1) * hd]
        return jnp.where(qmask, jnp.concatenate([qh, qh], axis=0), 0.0).astype(BF16)

    qrow = lax.broadcasted_iota(jnp.int32, (rows, tk), 0) % ls
    kcol = lax.broadcasted_iota(jnp.int32, (rows, tk), 1)
    rel = (past + qrow - (s * tk + kcol)).astype(F32)
    for h in range(heads):
        sl = slice(h * hd, (h + 1) * hd)
        kh = jnp.concatenate([kp[j][:, sl] for j in range(pps)], axis=0).astype(BF16)
        vh = jnp.concatenate([vp[j][:, sl] for j in range(pps)], axis=0).astype(BF16)
        sc = lax.dot_general(q_rows(h), kh, NT_DIMS, preferred_element_type=F32)
        sc = sc - _alibi_slope(h, heads) * rel
        m_new, l_new, acc_new = _online_update(sc, vh, m_ref[h], l_ref[h], acc_ref[h])
        m_ref[h] = m_new
        l_ref[h] = l_new
        acc_ref[h] = acc_new

    @pl.when(s == pl.num_programs(1) - 1)
    def _():
        nk = BF16_ROWS
        qr = lax.broadcasted_iota(jnp.int32, (rows, nk), 0) % ls
        kc = lax.broadcasted_iota(jnp.int32, (rows, nk), 1)
        reln = (qr - kc).astype(F32)
        ok = (kc < ls) & (kc <= qr)
        pad = jnp.zeros((nk - ls, hd), F32)
        for h in range(heads):
            sl = slice(h * hd, (h + 1) * hd)
            kh = jnp.concatenate([kn_ref[:, sl], pad], axis=0).astype(BF16)
            vh = jnp.concatenate([vn_ref[:, sl], pad], axis=0).astype(BF16)
            sc = lax.dot_general(q_rows(h), kh, NT_DIMS, preferred_element_type=F32)
            sc = jnp.where(ok, sc - _alibi_slope(h, heads) * reln, -jnp.inf)
            _, l_new, acc_new = _online_update(sc, vh, m_ref[h], l_ref[h], acc_ref[h])
            on = acc_new / l_new
            o = on[:ls] - lam_ref[0] * on[ls:]
            o_ref[:, sl] = _sub_ln(o, g_ref[...], post_scale)


def _attn_sample(dm, layer, layer_scale, page_table, lam, u, cache_k, cache_v, subln, o_prev):
    pps, hd = dm.pps, 2 * dm.dk
    steps = dm.n_pages // pps
    row0 = dm.mp // dm.ls
    kern = functools.partial(_attn_sample_kernel, pps=pps, heads=dm.heads, dk=dm.dk,
                             page=dm.page, past=dm.past, ls=dm.ls, post_scale=layer_scale)

    def page_spec(j):
        return pl.BlockSpec(
            (None, None, dm.page, dm.aw),
            lambda b, s, pt: (layer, pt[b * dm.n_pages + s * pps + j], 0, 0))

    def row_spec(col):
        return pl.BlockSpec((dm.ls, dm.aw), lambda b, s, pt: (row0 + b, col))

    vmem = 2 * 2 * pps * dm.page * dm.aw * 4 + 6 * pps * dm.page * hd * 4 + (4 << 20)
    grid_spec = pltpu.PrefetchScalarGridSpec(
        num_scalar_prefetch=1,
        grid=(dm.bs, steps),
        in_specs=[
            pl.BlockSpec(memory_space=pltpu.SMEM),
            row_spec(0), row_spec(1), row_spec(2),
            pl.BlockSpec((1, hd), lambda b, s, pt: (0, 0)),
            pl.BlockSpec(memory_space=pl.ANY),
        ] + [page_spec(j) for j in range(pps)] + [page_spec(j) for j in range(pps)],
        out_specs=pl.BlockSpec((dm.ls, dm.aw), lambda b, s, pt: (row0 + b, 0)),
        scratch_shapes=[pltpu.VMEM((dm.heads, 2 * dm.ls, 1), F32),
                        pltpu.VMEM((dm.heads, 2 * dm.ls, 1), F32),
                        pltpu.VMEM((dm.heads, 2 * dm.ls, hd), F32)],
    )
    return pl.pallas_call(
        kern,
        grid_spec=grid_spec,
        out_shape=jax.ShapeDtypeStruct((dm.m, dm.aw), F32),
        input_output_aliases={6: 0},
        compiler_params=_params(("parallel", "arbitrary"), vmem),
        name="attn_sample",
    )(page_table, lam, u, u, u, subln, o_prev, *([cache_k] * pps), *([cache_v] * pps))


def _ssd_kernel(xs_ref, bm_ref, cm_ref, z_ref, dt_ref, cw_ref, cb_ref, dtb_ref, alog_ref,
                dsk_ref, gain_ref, cbuf_ref, h0_ref, yprev_ref,
                y_ref, nconv_ref, hlast_ref, xext_ref, hst_ref, *, lc, sg, hpg, sp, ns, cw):
    del yprev_ref
    c = pl.program_id(1)
    di = xs_ref.shape[1]
    gn = bm_ref.shape[1]
    halo = cw - 1
    top = SUBLANES

    @pl.when(c == 0)
    def _():
        hst_ref[...] = h0_ref[...]
        xext_ref[top - halo:top, :] = cbuf_ref[...]

    raw = jnp.concatenate([xs_ref[...], bm_ref[...], cm_ref[...]], axis=1)
    xext_ref[top:top + lc, :] = raw
    w = cw_ref[...]
    conv = raw * w[cw - 1:cw]
    for j in range(halo):
        conv = conv + xext_ref[top - halo + j:top - halo + j + lc, :] * w[j:j + 1]
    conv = conv + cb_ref[...]
    tail = xext_ref[top + lc - halo:top + lc, :]
    nconv_ref[...] = tail
    xext_ref[top - halo:top, :] = tail

    act = _silu(conv)
    xsa = act[:, :di]
    bmat = act[:, di:di + gn]
    cmat = act[:, di + gn:]

    dt = jax.nn.softplus(dt_ref[...] + dtb_ref[...])
    dta = dt * (-jnp.exp(alog_ref[...]))
    row = lax.broadcasted_iota(jnp.int32, (lc, lc), 0)
    col = lax.broadcasted_iota(jnp.int32, (lc, lc), 1)
    causal = col <= row
    tri = causal.astype(BF16)
    acs = sum(jnp.dot(tri, part, preferred_element_type=F32) for part in _split3(dta))
    er = lax.broadcasted_iota(jnp.int32, (LANES, LANES), 0)
    ec = lax.broadcasted_iota(jnp.int32, (LANES, LANES), 1)
    eye = (er == ec).astype(BF16)
    acs_t = sum(lax.dot_general(eye, part, NT_DIMS, preferred_element_type=F32)
                for part in _split3(acs))
    last = acs[lc - 1:lc, :]
    eacs = jnp.exp(acs)
    elast = jnp.exp(last)
    dend = jnp.exp(last - acs)
    dsk = dsk_ref[...]

    lane = lax.broadcasted_iota(jnp.int32, (1, 2 * sp), 1)
    lo = lane < sp
    subl = lax.broadcasted_iota(jnp.int32, (2 * sp, 1), 0)
    slo = subl < sp

    def pair(cols, r0):
        return jnp.where(lo, cols[:, r0:r0 + 1], cols[:, r0 + 1:r0 + 2])

    ys = []
    for g in range(sg):
        bg = bmat[:, g * ns:(g + 1) * ns].astype(BF16)
        cg = cmat[:, g * ns:(g + 1) * ns].astype(BF16)
        cbm = lax.dot_general(cg, bg, NT_DIMS, preferred_element_type=F32)
        for pp in range(hpg // 2):
            pidx = g * (hpg // 2) + pp
            r0 = 2 * pidx
            xp = xsa[:, pidx * 2 * sp:(pidx + 1) * 2 * sp]
            xdt = xp * pair(dt, r0)
            y = xp * pair(dsk, r0)
            for r, keep in ((r0, lo), (r0 + 1, jnp.logical_not(lo))):
                seg = acs[:, r:r + 1] - acs_t[r:r + 1, :]
                dec = jnp.exp(jnp.where(causal, seg, -jnp.inf))
                y = y + jnp.dot((cbm * dec).astype(BF16),
                                jnp.where(keep, xdt, 0.0).astype(BF16),
                                preferred_element_type=F32)
            hp = hst_ref[pidx]
            y = y + lax.dot_general(cg, hp.astype(BF16), NT_DIMS,
                                    preferred_element_type=F32) * pair(eacs, r0)
            xw = (xdt * pair(dend, r0)).astype(BF16)
            st = lax.dot_general(xw, bg, TN_DIMS, preferred_element_type=F32)
            cdec = jnp.where(slo, elast[:, r0:r0 + 1], elast[:, r0 + 1:r0 + 2])
            hst_ref[pidx] = hp * cdec + st
            ys.append(y)

    y = jnp.concatenate(ys, axis=1) * _silu(z_ref[...])
    gw = di // sg
    for g in range(sg):
        sl = slice(g * gw, (g + 1) * gw)
        y_ref[:, sl] = _rms(y[:, sl], gain_ref[:, sl])

    @pl.when(c == pl.num_programs(1) - 1)
    def _():
        hlast_ref[...] = hst_ref[...]


def _ssd(dm, u, dtraw, conv_w, conv_b, dt_bias, a_log, d_skip, gain, conv_buf, h0, y_prev,
         *, nb, lseq, lc, row0):
    nc = lseq // lc
    di, gn = dm.di, dm.sg * dm.ns
    npairs = dm.sh // 2
    pr = 2 * dm.sp
    xs_blk = (4 * dm.aw) // di
    b_blk = (4 * dm.aw + di) // gn
    halo = dm.cw - 1
    kern = functools.partial(_ssd_kernel, lc=lc, sg=dm.sg, hpg=dm.sh // dm.sg, sp=dm.sp,
                             ns=dm.ns, cw=dm.cw)

    def rows(width, blk):
        return pl.BlockSpec((lc, width), lambda b, c: (row0 + b * nc + c, blk))

    def const(shape):
        return pl.BlockSpec(shape, lambda b, c: (0,) * len(shape))

    in_specs = [
        rows(di, xs_blk), rows(gn, b_blk), rows(gn, b_blk + 1), rows(di, 3 * dm.aw // di),
        rows(LANES, 0),
        const((dm.cw, dm.cd)), const((1, dm.cd)), const((1, LANES)), const((1, LANES)),
        const((1, LANES)), const((1, di)),
        pl.BlockSpec((None, halo, dm.cd), lambda b, c: (b, 0, 0)),
        pl.BlockSpec((None, npairs, pr, dm.ns), lambda b, c: (b, 0, 0, 0)),
    ]
    args = [u, u, u, u, dtraw, conv_w, conv_b, dt_bias, a_log, d_skip, gain, conv_buf, h0]
    aliases = {}
    if y_prev is not None:
        in_specs.append(pl.BlockSpec(memory_space=pl.ANY))
        args.append(y_prev)
        aliases = {len(args) - 1: 0}
    else:
        kern = functools.partial(_drop_arg, kern, len(args))
    vmem = 2 * lc * (2 * di + 2 * gn + LANES + di) * 4 + 4 * npairs * pr * dm.ns * 4 \
        + 24 * lc * dm.cd * 4 + (8 << 20)
    return pl.pallas_call(
        kern,
        grid=(nb, nc),
        in_specs=in_specs,
        out_specs=[
            pl.BlockSpec((lc, di), lambda b, c: (row0 + b * nc + c, 0)),
            pl.BlockSpec((None, halo, dm.cd), lambda b, c: (b, 0, 0)),
            pl.BlockSpec((None, npairs, pr, dm.ns), lambda b, c: (b, 0, 0, 0)),
        ],
        out_shape=[jax.ShapeDtypeStruct((dm.m, di), F32),
                   jax.ShapeDtypeStruct((nb, halo, dm.cd), F32),
                   jax.ShapeDtypeStruct((nb, npairs, pr, dm.ns), F32)],
        scratch_shapes=[pltpu.VMEM((SUBLANES + lc, dm.cd), F32),
                        pltpu.VMEM((npairs, pr, dm.ns), F32)],
        input_output_aliases=aliases,
        compiler_params=_params(("parallel", "arbitrary"), vmem),
        name="ssd_%d" % lc,
    )(*args)


def _drop_arg(kern, pos, *refs):
    return kern(*refs[:pos], None, *refs[pos:])


def _outproj_kernel(o_ref, y_ref, wa_ref, wb_ref, x_ref, out_ref):
    acc = jnp.dot(o_ref[...].astype(BF16), wa_ref[...], preferred_element_type=F32)
    acc = acc + jnp.dot(y_ref[...].astype(BF16), wb_ref[...], preferred_element_type=F32)
    out_ref[...] = x_ref[...] + acc


def _outproj(dm, o, y, w_out, x):
    tm, tn, d = dm.tm, dm.tn, dm.d
    ka = dm.aw
    kb = dm.di
    assert ka == kb
    vmem = 2 * (tm * ka * 4 * 2 + 2 * ka * tn * 2 + 2 * tm * tn * 4) + 2 * tm * ka * 2 + tm * tn * 8
    return pl.pallas_call(
        _outproj_kernel,
        grid=(dm.m // tm, d // tn),
        in_specs=[
            pl.BlockSpec((tm, ka), lambda m, n: (m, 0)),
            pl.BlockSpec((tm, kb), lambda m, n: (m, 0)),
            pl.BlockSpec((ka, tn), lambda m, n: (0, n)),
            pl.BlockSpec((kb, tn), lambda m, n: (1, n)),
            pl.BlockSpec((tm, tn), lambda m, n: (m, n)),
        ],
        out_specs=pl.BlockSpec((tm, tn), lambda m, n: (m, n)),
        out_shape=jax.ShapeDtypeStruct((dm.m, d), F32),
        compiler_params=_params(("parallel", "parallel"), vmem),
        name="outproj",
    )(o, y, w_out, w_out, x)


def _ffn_kernel(x_ref, g_ref, gate_ref, wg_ref, wu_ref, wd_ref, o_ref, hn_ref):
    first = (pl.program_id(1) == 0) & (pl.program_id(2) == 0)

    @pl.when(first)
    def _():
        x = x_ref[...]
        hn_ref[...] = _rms(x, g_ref[...]).astype(BF16)
        o_ref[...] = x

    h = hn_ref[...]
    a = _silu(jnp.dot(h, wg_ref[...], preferred_element_type=F32))
    a = a * jnp.dot(h, wu_ref[...], preferred_element_type=F32)
    if gate_ref is not None:
        a = a * gate_ref[...]
    o_ref[...] += jnp.dot(a.astype(BF16), wd_ref[...], preferred_element_type=F32)


def _ffn(dm, x, gain, wg, wu, wd, gates):
    tm, d = dm.tm, dm.d
    ne, _, f = wg.shape
    tf = _largest_divisor(f, dm.tf, LANES)
    in_specs = [
        pl.BlockSpec((tm, d), lambda m, e, j: (m, 0)),
        pl.BlockSpec((1, d), lambda m, e, j: (0, 0)),
    ]
    args = [x, gain]
    kern = _ffn_kernel
    if gates is not None:
        in_specs.append(pl.BlockSpec((None, tm, 1), lambda m, e, j: (e, m, 0)))
        args.append(gates)
    else:
        kern = functools.partial(_drop_arg, _ffn_kernel, 2)
    in_specs += [
        pl.BlockSpec((None, d, tf), lambda m, e, j: (e, 0, j)),
        pl.BlockSpec((None, d, tf), lambda m, e, j: (e, 0, j)),
        pl.BlockSpec((None, tf, d), lambda m, e, j: (e, j, 0)),
    ]
    args += [wg, wu, wd]
    vmem = 2 * (2 * tm * d * 4 + 3 * d * tf * 2 + tm * LANES * 4) + tm * d * 2 + 5 * tm * tf * 4 \
        + tm * d * 4
    return pl.pallas_call(
        kern,
        grid=(dm.m // tm, ne, f // tf),
        in_specs=in_specs,
        out_specs=pl.BlockSpec((tm, d), lambda m, e, j: (m, 0)),
        out_shape=jax.ShapeDtypeStruct((dm.m, d), F32),
        scratch_shapes=[pltpu.VMEM((tm, d), BF16)],
        compiler_params=_params(("parallel", "arbitrary", "arbitrary"), vmem),
        name="ffn_e%d" % ne,
    )(*args)


def _router_kernel(x_ref, g_ref, wh_ref, wl_ref, gates_ref, *, n_experts):
    xn = _rms(x_ref[...], g_ref[...])
    hi = xn.astype(BF16)
    lo = (xn - hi.astype(F32)).astype(BF16)
    wh = wh_ref[...]
    logits = (jnp.dot(hi, wh, preferred_element_type=F32)
              + jnp.dot(lo, wh, preferred_element_type=F32)
              + jnp.dot(hi, wl_ref[...], preferred_element_type=F32))
    lane = lax.broadcasted_iota(jnp.int32, logits.shape, 1)
    valid = lane < n_experts
    lg = jnp.where(valid, logits, -jnp.inf)
    e = jnp.exp(lg - jnp.max(lg, axis=-1, keepdims=True))
    p = e / jnp.sum(e, axis=-1, keepdims=True)
    p = jnp.where(valid, p, -1.0)
    p1 = jnp.max(p, axis=-1, keepdims=True)
    i1 = jnp.min(jnp.where(p == p1, lane, LANES), axis=-1, keepdims=True)
    rest = jnp.where(lane == i1, -1.0, p)
    p2 = jnp.max(rest, axis=-1, keepdims=True)
    i2 = jnp.min(jnp.where(rest == p2, lane, LANES), axis=-1, keepdims=True)
    den = p1 + p2
    gates_ref[...] = jnp.where(lane == i1, p1 / den, jnp.where(lane == i2, p2 / den, 0.0))


def _router(dm, x, gain, w_hi, w_lo, n_experts):
    tm, d = dm.tm, dm.d
    kern = functools.partial(_router_kernel, n_experts=n_experts)
    vmem = 2 * (tm * d * 4 + 2 * d * LANES * 2 + tm * LANES * 4) + 4 * tm * d * 4
    return pl.pallas_call(
        kern,
        grid=(dm.m // tm,),
        in_specs=[
            pl.BlockSpec((tm, d), lambda m: (m, 0)),
            pl.BlockSpec((1, d), lambda m: (0, 0)),
            pl.BlockSpec((d, LANES), lambda m: (0, 0)),
            pl.BlockSpec((d, LANES), lambda m: (0, 0)),
        ],
        out_specs=pl.BlockSpec((tm, LANES), lambda m: (m, 0)),
        out_shape=jax.ShapeDtypeStruct((dm.m, LANES), F32),
        compiler_params=_params(("parallel",), vmem),
        name="router",
    )(x, gain, w_hi, w_lo)


def _hi_lo(w):
    hi = w.astype(BF16)
    return hi, (w - hi.astype(F32)).astype(BF16)


def _pad_lanes(a):
    return jnp.pad(a, [(0, 0)] * (a.ndim - 1) + [(0, LANES - a.shape[-1])])


def _make_dims(x_prompt, x_sample, cache_k, page_table, state_conv, state_ssm, q_norm):
    bp, lp, d = x_prompt.shape
    bs, ls, _ = x_sample.shape
    _, _, page, heads, hd = cache_k.shape
    dk = q_norm.shape[-1]
    assert hd == 2 * dk == LANES
    aw = heads * hd
    cw = state_conv.shape[2] + 1
    cd = state_conv.shape[3]
    _, _, sh, sp, ns = state_ssm.shape
    di = sh * sp
    sg = (cd - di) // (2 * ns)
    assert aw + di == d and aw == di and 2 * sp == LANES and ns == LANES and (sh // sg) % 2 == 0
    in_main = 3 * aw + di + cd
    n_pages = page_table.shape[1]
    mp, ms = bp * lp, bs * ls
    m = mp + ms
    assert ls == SUBLANES and mp % ls == 0
    return Dims(
        d=d, bp=bp, lp=lp, bs=bs, ls=ls, mp=mp, ms=ms, m=m, heads=heads, dk=dk, aw=aw, di=di,
        sh=sh, sp=sp, sg=sg, ns=ns, cw=cw, cd=cd, in_main=in_main, past=n_pages * page,
        page=page, n_pages=n_pages,
        tm=_largest_divisor(m, 768, BF16_ROWS),
        tn=_largest_divisor(math.gcd(math.gcd(2 * aw, in_main), d), 512, LANES),
        tq=_largest_divisor(lp, 512, LANES),
        lc=128 if lp % 128 == 0 else lp,
        pps=_largest_divisor(n_pages, 8, 1),
        tf=256)


def kernel(x_prompt, x_sample, cache_k, cache_v, page_table, state_conv, state_ssm, norm_mix, w_in, q_norm, k_norm, lambda_q1, lambda_k1, lambda_q2, lambda_k2, attn_subln, conv_w, conv_b, dt_bias, a_log, d_skip, ssm_norm, w_out, norm_ffn, dense_w_gate, dense_w_up, dense_w_down, router_w, expert_w_gate, expert_w_up, expert_w_down):
    dm = _make_dims(x_prompt, x_sample, cache_k, page_table, state_conv, state_ssm, q_norm)
    depth = w_in.shape[0]
    n_pool = cache_k.shape[1]
    n_experts = router_w.shape[-1]
    hd = 2 * dm.dk
    npairs = dm.sh // 2

    x = jnp.concatenate([x_prompt.reshape(dm.mp, dm.d), x_sample.reshape(dm.ms, dm.d)], axis=0)
    ck = cache_k.reshape(depth, n_pool, dm.page, dm.aw)
    cv = cache_v.reshape(depth, n_pool, dm.page, dm.aw)
    pt = page_table.reshape(-1).astype(jnp.int32)
    gi = lax.broadcasted_iota(jnp.int32, (LANES, LANES), 0) // dm.dk
    gj = lax.broadcasted_iota(jnp.int32, (LANES, LANES), 1) // dm.dk
    bd = (gi == gj).astype(BF16)
    zero_conv = jnp.zeros((dm.bp, dm.cw - 1, dm.cd), F32)
    zero_h = jnp.zeros((dm.bp, npairs, 2 * dm.sp, dm.ns), F32)

    outs = [[] for _ in range(8)]
    for l in range(depth):
        lam_init = 0.8 - 0.6 * math.exp(-0.3 * l)
        lam = (jnp.exp(jnp.sum(lambda_q1[l] * lambda_k1[l]))
               - jnp.exp(jnp.sum(lambda_q2[l] * lambda_k2[l])) + lam_init).reshape(1).astype(F32)
        reps = 2 * dm.aw // dm.dk // 2
        qkg = jnp.concatenate([jnp.tile(q_norm[l] * (dm.dk ** -0.5), reps),
                               jnp.tile(k_norm[l], reps)]).reshape(1, 2 * dm.aw)
        w_main = w_in[l][:, :dm.in_main].astype(BF16)
        wdt_hi, wdt_lo = _hi_lo(_pad_lanes(w_in[l][:, dm.in_main:]))
        u, dtraw = _inproj(dm, x, norm_mix[l].reshape(1, dm.d), w_main, qkg, bd, wdt_hi, wdt_lo)

        subln = attn_subln[l].reshape(1, hd)
        o = _attn_prompt(dm, 1.0 - lam_init, lam, u, subln)
        o = _attn_sample(dm, l, 1.0 - lam_init, pt, lam, u, ck, cv, subln, o)

        ssd_args = (conv_w[l], conv_b[l].reshape(1, dm.cd), _pad_lanes(dt_bias[l].reshape(1, dm.sh)),
                    _pad_lanes(a_log[l].reshape(1, dm.sh)), _pad_lanes(d_skip[l].reshape(1, dm.sh)),
                    ssm_norm[l].reshape(1, dm.di))
        y, conv_p, h_p = _ssd(dm, u, dtraw, *ssd_args, zero_conv, zero_h, None,
                              nb=dm.bp, lseq=dm.lp, lc=dm.lc, row0=0)
        y, conv_s, h_s = _ssd(dm, u, dtraw, *ssd_args, state_conv[l],
                              state_ssm[l].reshape(dm.bs, npairs, 2 * dm.sp, dm.ns), y,
                              nb=dm.bs, lseq=dm.ls, lc=dm.ls, row0=dm.mp // dm.ls)

        x = _outproj(dm, o, y, w_out[l].astype(BF16), x)

        j = l // 2
        gain = norm_ffn[l].reshape(1, dm.d)
        if l % 2 == 0:
            x = _ffn(dm, x, gain, dense_w_gate[j:j + 1].astype(BF16), dense_w_up[j:j + 1].astype(BF16),
                     dense_w_down[j:j + 1].astype(BF16), None)
        else:
            r_hi, r_lo = _hi_lo(_pad_lanes(router_w[j]))
            gates = _router(dm, x, gain, r_hi, r_lo, n_experts)
            gates_t = jnp.transpose(gates[:, :n_experts])[:, :, None]
            x = _ffn(dm, x, gain, expert_w_gate[j].astype(BF16), expert_w_up[j].astype(BF16),
                     expert_w_down[j].astype(BF16), gates_t)

        kk = u[:, dm.aw:2 * dm.aw]
        vv = u[:, 2 * dm.aw:3 * dm.aw]
        outs[0].append(kk[:dm.mp].reshape(dm.bp, dm.lp, dm.heads, hd))
        outs[1].append(vv[:dm.mp].reshape(dm.bp, dm.lp, dm.heads, hd))
        outs[2].append(conv_p)
        outs[3].append(h_p.reshape(dm.bp, dm.sh, dm.sp, dm.ns))
        outs[4].append(kk[dm.mp:].reshape(dm.bs, dm.ls, dm.heads, hd))
        outs[5].append(vv[dm.mp:].reshape(dm.bs, dm.ls, dm.heads, hd))
        outs[6].append(conv_s)
        outs[7].append(h_s.reshape(dm.bs, dm.sh, dm.sp, dm.ns))

    y_prompt = x[:dm.mp].reshape(dm.bp, dm.lp, dm.d)
    y_sample = x[dm.mp:].reshape(dm.bs, dm.ls, dm.d)
    return (y_prompt, y_sample) + tuple(jnp.stack(o) for o in outs)
```

```python
import functools
import math
from typing import NamedTuple

import jax
import jax.numpy as jnp
from jax import lax
from jax.experimental import pallas as pl
from jax.experimental.pallas import tpu as pltpu

F32 = jnp.float32
BF16 = jnp.bfloat16
EPS = 1e-6
LANES = 128
SUBLANES = 8
BF16_ROWS = 16
V7X_SCOPED_VMEM_BYTES = 60000 * 1024

NT_DIMS = (((1,), (1,)), ((), ()))
TN_DIMS = (((0,), (0,)), ((), ()))


class Dims(NamedTuple):
    d: int
    bp: int
    lp: int
    bs: int
    ls: int
    mp: int
    ms: int
    m: int
    heads: int
    dk: int
    aw: int
    di: int
    sh: int
    sp: int
    sg: int
    ns: int
    cw: int
    cd: int
    in_main: int
    past: int
    page: int
    n_pages: int
    tm: int
    tn: int
    tq: int
    lc: int
    pps: int
    tf: int
    tmx: int


def _largest_divisor(n, cap, mult):
    best = None
    for t in range(mult, min(n, cap) + 1, mult):
        if n % t == 0:
            best = t
    assert best is not None, (n, cap, mult)
    return best


def _params(sem, vmem_bytes):
    return pltpu.CompilerParams(
        dimension_semantics=sem,
        vmem_limit_bytes=int(min(vmem_bytes, V7X_SCOPED_VMEM_BYTES)))


def _split3(x):
    hi = x.astype(BF16)
    r = x - hi.astype(F32)
    mid = r.astype(BF16)
    lo = (r - mid.astype(F32)).astype(BF16)
    return hi, mid, lo


def _rms(x, gain):
    ms = jnp.mean(x * x, axis=-1, keepdims=True)
    return x * lax.rsqrt(ms + EPS) * gain


def _silu(x):
    return x * jax.nn.sigmoid(x)


def _inproj_kernel(x_ref, g_ref, w_ref, qkg_ref, bd_ref, wdh_ref, wdl_ref,
                   u_ref, dt_ref, xn_ref, *, n_qk_tiles, tn, dk):
    n = pl.program_id(1)

    @pl.when(n == 0)
    def _():
        xn = _rms(x_ref[...], g_ref[...])
        hi = xn.astype(BF16)
        lo = (xn - hi.astype(F32)).astype(BF16)
        xn_ref[...] = hi
        wh = wdh_ref[...]
        dt_ref[...] = (jnp.dot(hi, wh, preferred_element_type=F32)
                       + jnp.dot(lo, wh, preferred_element_type=F32)
                       + jnp.dot(hi, wdl_ref[...], preferred_element_type=F32))

    acc = jnp.dot(xn_ref[...], w_ref[...], preferred_element_type=F32)

    @pl.when(n < n_qk_tiles)
    def _():
        bd = bd_ref[...]
        for j in range(tn // LANES):
            sl = slice(j * LANES, (j + 1) * LANES)
            a = acc[:, sl]
            sq = a * a
            hi = sq.astype(BF16)
            lo = (sq - hi.astype(F32)).astype(BF16)
            ss = (jnp.dot(hi, bd, preferred_element_type=F32)
                  + jnp.dot(lo, bd, preferred_element_type=F32))
            u_ref[:, sl] = a * lax.rsqrt(ss * (1.0 / dk) + EPS) * qkg_ref[:, sl]

    @pl.when(n >= n_qk_tiles)
    def _():
        u_ref[...] = acc


def _inproj(dm, x, gain, w_main, qkg, bd, wdt_hi, wdt_lo):
    tm, tn, d = dm.tm, dm.tn, dm.d
    n_tiles = dm.in_main // tn
    n_qk = 2 * dm.aw // tn
    kern = functools.partial(_inproj_kernel, n_qk_tiles=n_qk, tn=tn, dk=dm.dk)
    vmem = 2 * (tm * d * 4 + d * tn * 2 + tm * tn * 4 + tm * LANES * 4 + 2 * d * LANES * 2) \
        + tm * d * 2 + 3 * tm * d * 4
    return pl.pallas_call(
        kern,
        grid=(dm.m // tm, n_tiles),
        in_specs=[
            pl.BlockSpec((tm, d), lambda m, n: (m, 0)),
            pl.BlockSpec((1, d), lambda m, n: (0, 0)),
            pl.BlockSpec((d, tn), lambda m, n: (0, n)),
            pl.BlockSpec((1, tn), lambda m, n: (0, jnp.minimum(n, n_qk - 1))),
            pl.BlockSpec((LANES, LANES), lambda m, n: (0, 0)),
            pl.BlockSpec((d, LANES), lambda m, n: (0, 0)),
            pl.BlockSpec((d, LANES), lambda m, n: (0, 0)),
        ],
        out_specs=[
            pl.BlockSpec((tm, tn), lambda m, n: (m, n)),
            pl.BlockSpec((tm, LANES), lambda m, n: (m, 0)),
        ],
        out_shape=[jax.ShapeDtypeStruct((dm.m, dm.in_main), F32),
                   jax.ShapeDtypeStruct((dm.m, LANES), F32)],
        scratch_shapes=[pltpu.VMEM((tm, d), BF16)],
        compiler_params=_params(("parallel", "arbitrary"), vmem),
        name="inproj",
    )(x, gain, w_main, qkg, bd, wdt_hi, wdt_lo)


def _online_update(sc, v_bf, m_prev, l_prev, acc_prev):
    m_new = jnp.maximum(m_prev, jnp.max(sc, axis=-1, keepdims=True))
    alpha = jnp.exp(m_prev - m_new)
    p = jnp.exp(sc - m_new)
    l_new = alpha * l_prev + jnp.sum(p, axis=-1, keepdims=True)
    acc_new = alpha * acc_prev + jnp.dot(p.astype(BF16), v_bf, preferred_element_type=F32)
    return m_new, l_new, acc_new


def _sub_ln(o, gain, post_scale):
    return _rms(o, gain) * post_scale


def _alibi_slope(h, heads):
    return 2.0 ** (-8.0 * (h + 1) / heads)


BF16_EXACT_INT = 256


def _pos_split(idx):
    lo = idx % BF16_EXACT_INT
    return lo.astype(F32), (idx - lo).astype(F32)


def _attn_prompt_kernel(lam_ref, q_ref, k_ref, v_ref, g_ref, o_ref,
                        qa_ref, ka_ref, vo_ref, m_ref, acc_ref,
                        *, tq, dk, heads, hb, post_scale):
    hblk = pl.program_id(1)
    qi = pl.program_id(2)
    ki = pl.program_id(3)
    hd = 2 * dk
    lane = lax.broadcasted_iota(jnp.int32, (tq, hd), 1)
    tok = lax.broadcasted_iota(jnp.int32, (tq, hd), 0)
    t_lo, t_hi = _pos_split(tok)

    def features(off, f0, f1, f2, f3, f4):
        return jnp.where(lane == off, f0,
                         jnp.where(lane == off + 1, f1,
                                   jnp.where(lane == off + 2, f2,
                                             jnp.where(lane == off + 3, f3,
                                                       jnp.where(lane == off + 4, f4, 0.0)))))

    @pl.when(ki == 0)
    def _():
        for j in range(hb):
            slope = jnp.exp2(-8.0 * (hblk * hb + j + 1).astype(F32) / heads)
            q = q_ref[:, j * hd:(j + 1) * hd]
            args = (-slope * t_lo, -slope * t_hi, slope, slope, -slope)
            qa_ref[2 * j] = jnp.where(lane < dk, q, features(dk, *args)).astype(BF16)
            qa_ref[2 * j + 1] = jnp.where(lane >= dk, q, features(0, *args)).astype(BF16)
        m_ref[...] = jnp.full(m_ref.shape, -jnp.inf, F32)
        acc_ref[...] = jnp.zeros(acc_ref.shape, F32)

    def step(diagonal):
        base = ((qi - ki) * tq).astype(F32)
        args = (1.0, 1.0, t_lo, t_hi, base)
        kf_a = features(dk, *args)
        kf_b = features(0, *args)
        ones = jnp.ones((tq, hd), BF16)
        if diagonal:
            row = lax.broadcasted_iota(jnp.int32, (tq, tq), 0)
            col = lax.broadcasted_iota(jnp.int32, (tq, tq), 1)
            visible = col <= row
        for j in range(hb):
            sl = slice(j * hd, (j + 1) * hd)
            k = k_ref[:, sl]
            ka_ref[2 * j] = jnp.where(lane < dk, k, kf_a).astype(BF16)
            ka_ref[2 * j + 1] = jnp.where(lane >= dk, k, kf_b).astype(BF16)
            vo_ref[j] = jnp.concatenate([v_ref[:, sl].astype(BF16), ones], axis=1)
        for i in range(2 * hb):
            sc = lax.dot_general(qa_ref[i], ka_ref[i], NT_DIMS, preferred_element_type=F32)
            if diagonal:
                sc = jnp.where(visible, sc, -jnp.inf)
            m_prev = m_ref[i]
            m_new = jnp.maximum(m_prev, jnp.max(sc, axis=-1, keepdims=True))
            p = jnp.exp(sc - m_new).astype(BF16)
            pv = jnp.dot(p, vo_ref[i // 2], preferred_element_type=F32)
            acc_ref[i] = jnp.exp(m_prev - m_new) * acc_ref[i] + pv
            m_ref[i] = m_new

    @pl.when(ki < qi)
    def _():
        step(False)

    @pl.when(ki == qi)
    def _():
        step(True)
        for j in range(hb):
            a1 = acc_ref[2 * j]
            a2 = acc_ref[2 * j + 1]
            o = a1[:, :hd] / a1[:, hd:] - lam_ref[0] * (a2[:, :hd] / a2[:, hd:])
            o_ref[:, j * hd:(j + 1) * hd] = _sub_ln(o, g_ref[...], post_scale)


def _attn_prompt(dm, layer_scale, lam, u, subln):
    tq, hd, hb = dm.tq, 2 * dm.dk, 2
    nq = dm.lp // tq
    nhb = dm.heads // hb
    assert 8 % dm.heads == 0 and tq & (tq - 1) == 0 and nq <= BF16_EXACT_INT and dm.dk >= 5
    kern = functools.partial(_attn_prompt_kernel, tq=tq, dk=dm.dk, heads=dm.heads, hb=hb,
                             post_scale=layer_scale)
    vmem = hb * (2 * 4 * tq * hd * 4 + 8 * tq * hd * 2 + 4 * tq * hd * 4 + 8 * tq * tq * 4) + (4 << 20)
    return pl.pallas_call(
        kern,
        grid=(dm.bp, nhb, nq, nq),
        in_specs=[
            pl.BlockSpec(memory_space=pltpu.SMEM),
            pl.BlockSpec((tq, hb * hd), lambda b, h, qi, ki: (b * nq + qi, h)),
            pl.BlockSpec((tq, hb * hd), lambda b, h, qi, ki: (b * nq + jnp.minimum(ki, qi), nhb + h)),
            pl.BlockSpec((tq, hb * hd), lambda b, h, qi, ki: (b * nq + jnp.minimum(ki, qi), 2 * nhb + h)),
            pl.BlockSpec((1, hd), lambda b, h, qi, ki: (0, 0)),
        ],
        out_specs=pl.BlockSpec((tq, hb * hd), lambda b, h, qi, ki: (b * nq + qi, h)),
        out_shape=jax.ShapeDtypeStruct((dm.mp, dm.aw), F32),
        scratch_shapes=[pltpu.VMEM((2 * hb, tq, hd), BF16),
                        pltpu.VMEM((2 * hb, tq, hd), BF16),
                        pltpu.VMEM((hb, tq, 2 * hd), BF16),
                        pltpu.VMEM((2 * hb, tq, 1), F32),
                        pltpu.VMEM((2 * hb, tq, 2 * hd), F32)],
        compiler_params=_params(("parallel", "parallel", "parallel", "arbitrary"), vmem),
        name="attn_prompt",
    )(lam, u, u, u, subln)


def _attn_sample_kernel(pt_ref, lam_ref, q_ref, kn_ref, vn_ref, g_ref, *rest,
                        pps, heads, dk, page, past, ls, post_scale):
    del pt_ref
    kp = rest[:pps]
    vp = rest[pps:2 * pps]
    o_ref, m_ref, l_ref, acc_ref = rest[2 * pps:]
    s = pl.program_id(1)
    hd = 2 * dk
    rows = 2 * ls
    tk = pps * page

    @pl.when(s == 0)
    def _():
        m_ref[...] = jnp.full(m_ref.shape, -jnp.inf, F32)
        l_ref[...] = jnp.zeros(l_ref.shape, F32)
        acc_ref[...] = jnp.zeros(acc_ref.shape, F32)

    rown = lax.broadcasted_iota(jnp.int32, (rows, hd), 0)
    lane = lax.broadcasted_iota(jnp.int32, (rows, hd), 1)
    qmask = (rown < ls) == (lane < dk)

    def q_rows(h):
        qh = q_ref[:, h * hd:(h + 1) * hd]
        return jnp.where(qmask, jnp.concatenate([qh, qh], axis=0), 0.0).astype(BF16)

    qrow = lax.broadcasted_iota(jnp.int32, (rows, tk), 0) % ls
    kcol = lax.broadcasted_iota(jnp.int32, (rows, tk), 1)
    rel = (past + qrow - (s * tk + kcol)).astype(F32)
    for h in range(heads):
        sl = slice(h * hd, (h + 1) * hd)
        kh = jnp.concatenate([kp[j][:, h, :] for j in range(pps)], axis=0).astype(BF16)
        vh = jnp.concatenate([vp[j][:, h, :] for j in range(pps)], axis=0).astype(BF16)
        sc = lax.dot_general(q_rows(h), kh, NT_DIMS, preferred_element_type=F32)
        sc = sc - _alibi_slope(h, heads) * rel
        m_new, l_new, acc_new = _online_update(sc, vh, m_ref[h], l_ref[h], acc_ref[h])
        m_ref[h] = m_new
        l_ref[h] = l_new
        acc_ref[h] = acc_new

    @pl.when(s == pl.num_programs(1) - 1)
    def _():
        nk = BF16_ROWS
        qr = lax.broadcasted_iota(jnp.int32, (rows, nk), 0) % ls
        kc = lax.broadcasted_iota(jnp.int32, (rows, nk), 1)
        reln = (qr - kc).astype(F32)
        ok = (kc < ls) & (kc <= qr)
        pad = jnp.zeros((nk - ls, hd), F32)
        for h in range(heads):
            sl = slice(h * hd, (h + 1) * hd)
            kh = jnp.concatenate([kn_ref[:, sl], pad], axis=0).astype(BF16)
            vh = jnp.concatenate([vn_ref[:, sl], pad], axis=0).astype(BF16)
            sc = lax.dot_general(q_rows(h), kh, NT_DIMS, preferred_element_type=F32)
            sc = jnp.where(ok, sc - _alibi_slope(h, heads) * reln, -jnp.inf)
            _, l_new, acc_new = _online_update(sc, vh, m_ref[h], l_ref[h], acc_ref[h])
            on = acc_new / l_new
            o = on[:ls] - lam_ref[0] * on[ls:]
            o_ref[:, sl] = _sub_ln(o, g_ref[...], post_scale)


def _attn_sample(dm, layer, layer_scale, page_table, lam, u, cache_k, cache_v, subln):
    pps, hd = dm.pps, 2 * dm.dk
    steps = dm.n_pages // pps
    row0 = dm.mp // dm.ls
    kern = functools.partial(_attn_sample_kernel, pps=pps, heads=dm.heads, dk=dm.dk,
                             page=dm.page, past=dm.past, ls=dm.ls, post_scale=layer_scale)

    def page_spec(j):
        return pl.BlockSpec(
            (None, None, dm.page, dm.heads, hd),
            lambda b, s, pt: (layer, pt[b * dm.n_pages + s * pps + j], 0, 0, 0))

    def row_spec(col):
        return pl.BlockSpec((dm.ls, dm.aw), lambda b, s, pt: (row0 + b, col))

    vmem = 2 * 2 * pps * dm.page * dm.aw * 4 + 6 * pps * dm.page * hd * 4 + (4 << 20)
    grid_spec = pltpu.PrefetchScalarGridSpec(
        num_scalar_prefetch=1,
        grid=(dm.bs, steps),
        in_specs=[
            pl.BlockSpec(memory_space=pltpu.SMEM),
            row_spec(0), row_spec(1), row_spec(2),
            pl.BlockSpec((1, hd), lambda b, s, pt: (0, 0)),
        ] + [page_spec(j) for j in range(pps)] + [page_spec(j) for j in range(pps)],
        out_specs=pl.BlockSpec((dm.ls, dm.aw), lambda b, s, pt: (b, 0)),
        scratch_shapes=[pltpu.VMEM((dm.heads, 2 * dm.ls, 1), F32),
                        pltpu.VMEM((dm.heads, 2 * dm.ls, 1), F32),
                        pltpu.VMEM((dm.heads, 2 * dm.ls, hd), F32)],
    )
    return pl.pallas_call(
        kern,
        grid_spec=grid_spec,
        out_shape=jax.ShapeDtypeStruct((dm.ms, dm.aw), F32),
        compiler_params=_params(("parallel", "arbitrary"), vmem),
        name="attn_sample",
    )(page_table, lam, u, u, u, subln, *([cache_k] * pps), *([cache_v] * pps))


def _ssd_kernel(xs_ref, bm_ref, cm_ref, z_ref, dt_ref, cw_ref, cb_ref, dtb_ref, alog_ref,
                dsk_ref, gain_ref, cbuf_ref, h0_ref,
                y_ref, nconv_ref, hlast_ref, xext_ref, hst_ref, *, lc, sg, hpg, sp, ns, cw):
    c = pl.program_id(1)
    di = xs_ref.shape[1]
    gn = bm_ref.shape[1]
    halo = cw - 1
    top = SUBLANES

    @pl.when(c == 0)
    def _():
        hst_ref[...] = h0_ref[...]
        xext_ref[top - halo:top, :] = cbuf_ref[...]

    raw = jnp.concatenate([xs_ref[...], bm_ref[...], cm_ref[...]], axis=1)
    xext_ref[top:top + lc, :] = raw
    w = cw_ref[...]
    conv = raw * w[cw - 1:cw]
    for j in range(halo):
        conv = conv + xext_ref[top - halo + j:top - halo + j + lc, :] * w[j:j + 1]
    conv = conv + cb_ref[...]
    tail = xext_ref[top + lc - halo:top + lc, :]
    nconv_ref[...] = tail
    xext_ref[top - halo:top, :] = tail

    act = _silu(conv)
    xsa = act[:, :di]
    bmat = act[:, di:di + gn]
    cmat = act[:, di + gn:]

    dt = jax.nn.softplus(dt_ref[...] + dtb_ref[...])
    dta = dt * (-jnp.exp(alog_ref[...]))
    row = lax.broadcasted_iota(jnp.int32, (lc, lc), 0)
    col = lax.broadcasted_iota(jnp.int32, (lc, lc), 1)
    causal = col <= row
    tri = causal.astype(BF16)
    acs = sum(jnp.dot(tri, part, preferred_element_type=F32) for part in _split3(dta))
    er = lax.broadcasted_iota(jnp.int32, (LANES, LANES), 0)
    ec = lax.broadcasted_iota(jnp.int32, (LANES, LANES), 1)
    eye = (er == ec).astype(BF16)
    acs_t = sum(lax.dot_general(eye, part, NT_DIMS, preferred_element_type=F32)
                for part in _split3(acs))
    last = acs[lc - 1:lc, :]
    eacs = jnp.exp(acs)
    elast = jnp.exp(last)
    dend = jnp.exp(last - acs)
    dsk = dsk_ref[...]

    lane = lax.broadcasted_iota(jnp.int32, (1, 2 * sp), 1)
    lo = lane < sp
    subl = lax.broadcasted_iota(jnp.int32, (2 * sp, 1), 0)
    slo = subl < sp

    def pair(cols, r0):
        return jnp.where(lo, cols[:, r0:r0 + 1], cols[:, r0 + 1:r0 + 2])

    ys = []
    for g in range(sg):
        bg = bmat[:, g * ns:(g + 1) * ns].astype(BF16)
        cg = cmat[:, g * ns:(g + 1) * ns].astype(BF16)
        cbm = lax.dot_general(cg, bg, NT_DIMS, preferred_element_type=F32)
        for pp in range(hpg // 2):
            pidx = g * (hpg // 2) + pp
            r0 = 2 * pidx
            xp = xsa[:, pidx * 2 * sp:(pidx + 1) * 2 * sp]
            xdt = xp * pair(dt, r0)
            y = xp * pair(dsk, r0)
            for r, keep in ((r0, lo), (r0 + 1, jnp.logical_not(lo))):
                seg = acs[:, r:r + 1] - acs_t[r:r + 1, :]
                dec = jnp.exp(jnp.where(causal, seg, -jnp.inf))
                y = y + jnp.dot((cbm * dec).astype(BF16),
                                jnp.where(keep, xdt, 0.0).astype(BF16),
                                preferred_element_type=F32)
            hp = hst_ref[pidx]
            y = y + lax.dot_general(cg, hp.astype(BF16), NT_DIMS,
                                    preferred_element_type=F32) * pair(eacs, r0)
            xw = (xdt * pair(dend, r0)).astype(BF16)
            st = lax.dot_general(xw, bg, TN_DIMS, preferred_element_type=F32)
            cdec = jnp.where(slo, elast[:, r0:r0 + 1], elast[:, r0 + 1:r0 + 2])
            hst_ref[pidx] = hp * cdec + st
            ys.append(y)

    y = jnp.concatenate(ys, axis=1) * _silu(z_ref[...])
    gw = di // sg
    for g in range(sg):
        sl = slice(g * gw, (g + 1) * gw)
        y_ref[:, sl] = _rms(y[:, sl], gain_ref[:, sl])

    @pl.when(c == pl.num_programs(1) - 1)
    def _():
        hlast_ref[...] = hst_ref[...]


def _ssd(dm, u, dtraw, conv_w, conv_b, dt_bias, a_log, d_skip, gain, conv_buf, h0,
         *, nb, lseq, lc, row0):
    nc = lseq // lc
    di, gn = dm.di, dm.sg * dm.ns
    npairs = dm.sh // 2
    pr = 2 * dm.sp
    xs_blk = (4 * dm.aw) // di
    b_blk = (4 * dm.aw + di) // gn
    halo = dm.cw - 1
    kern = functools.partial(_ssd_kernel, lc=lc, sg=dm.sg, hpg=dm.sh // dm.sg, sp=dm.sp,
                             ns=dm.ns, cw=dm.cw)

    def rows(width, blk):
        return pl.BlockSpec((lc, width), lambda b, c: (row0 + b * nc + c, blk))

    def const(shape):
        return pl.BlockSpec(shape, lambda b, c: (0,) * len(shape))

    in_specs = [
        rows(di, xs_blk), rows(gn, b_blk), rows(gn, b_blk + 1), rows(di, 3 * dm.aw // di),
        rows(LANES, 0),
        const((dm.cw, dm.cd)), const((1, dm.cd)), const((1, LANES)), const((1, LANES)),
        const((1, LANES)), const((1, di)),
        pl.BlockSpec((None, halo, dm.cd), lambda b, c: (b, 0, 0)),
        pl.BlockSpec((None, npairs, pr, dm.ns), lambda b, c: (b, 0, 0, 0)),
    ]
    args = [u, u, u, u, dtraw, conv_w, conv_b, dt_bias, a_log, d_skip, gain, conv_buf, h0]
    vmem = 2 * lc * (2 * di + 2 * gn + LANES + di) * 4 + 4 * npairs * pr * dm.ns * 4 \
        + 24 * lc * dm.cd * 4 + (8 << 20)
    return pl.pallas_call(
        kern,
        grid=(nb, nc),
        in_specs=in_specs,
        out_specs=[
            pl.BlockSpec((lc, di), lambda b, c: (b * nc + c, 0)),
            pl.BlockSpec((None, halo, dm.cd), lambda b, c: (b, 0, 0)),
            pl.BlockSpec((None, npairs, pr, dm.ns), lambda b, c: (b, 0, 0, 0)),
        ],
        out_shape=[jax.ShapeDtypeStruct((nb * lseq, di), F32),
                   jax.ShapeDtypeStruct((nb, halo, dm.cd), F32),
                   jax.ShapeDtypeStruct((nb, npairs, pr, dm.ns), F32)],
        scratch_shapes=[pltpu.VMEM((SUBLANES + lc, dm.cd), F32),
                        pltpu.VMEM((npairs, pr, dm.ns), F32)],
        compiler_params=_params(("parallel", "arbitrary"), vmem),
        name="ssd_%d" % lc,
    )(*args)


def _outproj_kernel(op_ref, yp_ref, os_ref, ys_ref, wa_ref, wb_ref, x_ref, out_ref,
                    *, n_prompt_tiles, ms):
    m = pl.program_id(0)

    def project(o, y, rows):
        acc = jnp.dot(o.astype(BF16), wa_ref[...], preferred_element_type=F32)
        acc = acc + jnp.dot(y.astype(BF16), wb_ref[...], preferred_element_type=F32)
        out_ref[0:rows, :] = x_ref[0:rows, :] + acc

    @pl.when(m < n_prompt_tiles)
    def _():
        project(op_ref[...], yp_ref[...], op_ref.shape[0])

    @pl.when(m == n_prompt_tiles)
    def _():
        project(os_ref[...], ys_ref[...], ms)


def _outproj(dm, o_p, y_p, o_s, y_s, w_out, x):
    tn, d = dm.tn, dm.d
    ka = dm.aw
    kb = dm.di
    assert ka == kb
    tm = _largest_divisor(dm.mp, 512, BF16_ROWS)
    npt = dm.mp // tm
    assert dm.ms <= tm
    kern = functools.partial(_outproj_kernel, n_prompt_tiles=npt, ms=dm.ms)
    vmem = 2 * (tm * ka * 4 * 2 + 2 * dm.ms * ka * 4 + 2 * ka * tn * 2 + 2 * tm * tn * 4) \
        + 2 * tm * ka * 2 + tm * tn * 8
    return pl.pallas_call(
        kern,
        grid=(npt + 1, d // tn),
        in_specs=[
            pl.BlockSpec((tm, ka), lambda m, n: (jnp.minimum(m, npt - 1), 0)),
            pl.BlockSpec((tm, kb), lambda m, n: (jnp.minimum(m, npt - 1), 0)),
            pl.BlockSpec((dm.ms, ka), lambda m, n: (0, 0)),
            pl.BlockSpec((dm.ms, kb), lambda m, n: (0, 0)),
            pl.BlockSpec((ka, tn), lambda m, n: (0, n)),
            pl.BlockSpec((kb, tn), lambda m, n: (1, n)),
            pl.BlockSpec((tm, tn), lambda m, n: (m, n)),
        ],
        out_specs=pl.BlockSpec((tm, tn), lambda m, n: (m, n)),
        out_shape=jax.ShapeDtypeStruct((dm.m, d), F32),
        compiler_params=_params(("parallel", "parallel"), vmem),
        name="outproj",
    )(o_p, y_p, o_s, y_s, w_out, w_out, x)


def _ffn_kernel(x_ref, g_ref, wg_ref, wu_ref, wd_ref, o_ref, hn_ref):
    @pl.when(pl.program_id(1) == 0)
    def _():
        x = x_ref[...]
        hn_ref[...] = _rms(x, g_ref[...]).astype(BF16)
        o_ref[...] = x

    h = hn_ref[...]
    a = _silu(jnp.dot(h, wg_ref[...], preferred_element_type=F32))
    a = a * jnp.dot(h, wu_ref[...], preferred_element_type=F32)
    o_ref[...] += jnp.dot(a.astype(BF16), wd_ref[...], preferred_element_type=F32)


def _ffn(dm, x, gain, wg, wu, wd):
    tm, d = dm.tm, dm.d
    f = wg.shape[1]
    tf = _largest_divisor(f, dm.tf, LANES)
    vmem = 2 * (2 * tm * d * 4 + 3 * d * tf * 2) + tm * d * 2 + 5 * tm * tf * 4 + tm * d * 4
    return pl.pallas_call(
        _ffn_kernel,
        grid=(dm.m // tm, f // tf),
        in_specs=[
            pl.BlockSpec((tm, d), lambda m, j: (m, 0)),
            pl.BlockSpec((1, d), lambda m, j: (0, 0)),
            pl.BlockSpec((d, tf), lambda m, j: (0, j)),
            pl.BlockSpec((d, tf), lambda m, j: (0, j)),
            pl.BlockSpec((tf, d), lambda m, j: (j, 0)),
        ],
        out_specs=pl.BlockSpec((tm, d), lambda m, j: (m, 0)),
        out_shape=jax.ShapeDtypeStruct((dm.m, d), F32),
        scratch_shapes=[pltpu.VMEM((tm, d), BF16)],
        compiler_params=_params(("parallel", "arbitrary"), vmem),
        name="ffn",
    )(x, gain, wg, wu, wd)


def _router_kernel(x_ref, g_ref, wh_ref, wl_ref, gates_ref, *, n_experts):
    xn = _rms(x_ref[...], g_ref[...])
    hi = xn.astype(BF16)
    lo = (xn - hi.astype(F32)).astype(BF16)
    wh = wh_ref[...]
    logits = (jnp.dot(hi, wh, preferred_element_type=F32)
              + jnp.dot(lo, wh, preferred_element_type=F32)
              + jnp.dot(hi, wl_ref[...], preferred_element_type=F32))
    lane = lax.broadcasted_iota(jnp.int32, logits.shape, 1)
    valid = lane < n_experts
    lg = jnp.where(valid, logits, -jnp.inf)
    e = jnp.exp(lg - jnp.max(lg, axis=-1, keepdims=True))
    p = e / jnp.sum(e, axis=-1, keepdims=True)
    p = jnp.where(valid, p, -1.0)
    p1 = jnp.max(p, axis=-1, keepdims=True)
    i1 = jnp.min(jnp.where(p == p1, lane, LANES), axis=-1, keepdims=True)
    rest = jnp.where(lane == i1, -1.0, p)
    p2 = jnp.max(rest, axis=-1, keepdims=True)
    i2 = jnp.min(jnp.where(rest == p2, lane, LANES), axis=-1, keepdims=True)
    den = p1 + p2
    gates_ref[...] = jnp.where(
        lane == 0, i1.astype(F32),
        jnp.where(lane == 1, i2.astype(F32),
                  jnp.where(lane == 2, p1 / den, jnp.where(lane == 3, p2 / den, 0.0))))


def _router(dm, x, gain, w_hi, w_lo, n_experts):
    tm, d = dm.tm, dm.d
    kern = functools.partial(_router_kernel, n_experts=n_experts)
    vmem = 2 * (tm * d * 4 + 2 * d * LANES * 2 + tm * LANES * 4) + 4 * tm * d * 4
    return pl.pallas_call(
        kern,
        grid=(dm.m // tm,),
        in_specs=[
            pl.BlockSpec((tm, d), lambda m: (m, 0)),
            pl.BlockSpec((1, d), lambda m: (0, 0)),
            pl.BlockSpec((d, LANES), lambda m: (0, 0)),
            pl.BlockSpec((d, LANES), lambda m: (0, 0)),
        ],
        out_specs=pl.BlockSpec((tm, LANES), lambda m: (m, 0)),
        out_shape=jax.ShapeDtypeStruct((dm.m, LANES), F32),
        compiler_params=_params(("parallel",), vmem),
        name="router",
    )(x, gain, w_hi, w_lo)


def _moe_plan(route, n_experts, tmx, n_tiles):
    m = route.shape[0]
    e_flat = route[:, 0:2].astype(jnp.int32).reshape(-1)
    g_flat = route[:, 2:4].reshape(-1)
    onehot = (e_flat[:, None] == jnp.arange(n_experts, dtype=jnp.int32)[None, :]).astype(jnp.int32)
    csum = jnp.cumsum(onehot, axis=0)
    rank = jnp.sum((csum - onehot) * onehot, axis=1)
    counts = csum[-1]
    tiles_per = (counts + tmx - 1) // tmx
    tile_end = jnp.cumsum(tiles_per)
    tile_start = tile_end - tiles_per
    slot = jnp.take(tile_start, e_flat) * tmx + rank
    token = jnp.arange(2 * m, dtype=jnp.int32) // 2
    slot_token = jnp.zeros((n_tiles * tmx,), jnp.int32).at[slot].set(token)
    slot_gate = jnp.zeros((n_tiles * tmx,), F32).at[slot].set(g_flat)
    n_active = tile_end[-1]
    tile_ids = jnp.arange(n_tiles, dtype=jnp.int32)
    last_active = jnp.minimum(tile_ids, n_active - 1)
    tile_expert = jnp.sum((last_active[:, None] >= tile_end[None, :]).astype(jnp.int32), axis=1)
    return (tile_expert.astype(jnp.int32), slot_token, n_active.reshape(1).astype(jnp.int32),
            slot_gate.reshape(-1, 1), slot.astype(jnp.int32))


def _moe_ffn_kernel(te_ref, tok_ref, na_ref, x_hbm, g_ref, gate_ref, wg_ref, wu_ref, wd_ref,
                    o_ref, xbuf_ref, sem_ref, hn_ref, *, tmx):
    del te_ref
    t = pl.program_id(0)
    f = pl.program_id(1)
    n_active = na_ref[0]
    active = t < n_active
    slot = t % 2

    def start_rows(tile, buf):
        def body(i, carry):
            tok = tok_ref[tile * tmx + i]
            pltpu.make_async_copy(x_hbm.at[pl.ds(tok, 1)], xbuf_ref.at[buf, pl.ds(i, 1)],
                                  sem_ref.at[buf]).start()
            return carry
        lax.fori_loop(0, tmx, body, 0)

    @pl.when(active & (f == 0))
    def _():
        @pl.when(t == 0)
        def _():
            start_rows(0, 0)

        @pl.when(t + 1 < n_active)
        def _():
            start_rows(t + 1, 1 - slot)

        pltpu.make_async_copy(xbuf_ref.at[slot], xbuf_ref.at[slot], sem_ref.at[slot]).wait()
        hn_ref[...] = _rms(xbuf_ref[slot], g_ref[...]).astype(BF16)
        o_ref[...] = jnp.zeros(o_ref.shape, F32)

    @pl.when(jnp.logical_not(active) & (f == 0))
    def _():
        o_ref[...] = jnp.zeros(o_ref.shape, F32)

    @pl.when(active)
    def _():
        h = hn_ref[...]
        a = _silu(jnp.dot(h, wg_ref[...], preferred_element_type=F32))
        a = a * jnp.dot(h, wu_ref[...], preferred_element_type=F32)
        o_ref[...] += jnp.dot(a.astype(BF16), wd_ref[...], preferred_element_type=F32)

    @pl.when(active & (f == pl.num_programs(1) - 1))
    def _():
        o_ref[...] = o_ref[...] * gate_ref[...]


def _moe_ffn(dm, x, gain, wg, wu, wd, plan, tmx, n_tiles):
    tile_expert, slot_token, n_active, slot_gate, _ = plan
    d = dm.d
    _, _, fe = wg.shape
    tf = _largest_divisor(fe, dm.tf, LANES)
    nf = fe // tf
    kern = functools.partial(_moe_ffn_kernel, tmx=tmx)

    def fidx(t, f, na):
        return jnp.where(t < na[0], f, nf - 1)

    vmem = 2 * tmx * d * 4 + 2 * tmx * d * 4 + tmx * d * 2 + 2 * 3 * d * tf * 2 + 5 * tmx * tf * 4 \
        + 2 * tmx * d * 4
    grid_spec = pltpu.PrefetchScalarGridSpec(
        num_scalar_prefetch=3,
        grid=(n_tiles, nf),
        in_specs=[
            pl.BlockSpec(memory_space=pl.ANY),
            pl.BlockSpec((1, d), lambda t, f, te, tok, na: (0, 0)),
            pl.BlockSpec((tmx, 1), lambda t, f, te, tok, na: (t, 0)),
            pl.BlockSpec((None, d, tf), lambda t, f, te, tok, na: (te[t], 0, fidx(t, f, na))),
            pl.BlockSpec((None, d, tf), lambda t, f, te, tok, na: (te[t], 0, fidx(t, f, na))),
            pl.BlockSpec((None, tf, d), lambda t, f, te, tok, na: (te[t], fidx(t, f, na), 0)),
        ],
        out_specs=pl.BlockSpec((tmx, d), lambda t, f, te, tok, na: (t, 0)),
        scratch_shapes=[pltpu.VMEM((2, tmx, d), F32),
                        pltpu.SemaphoreType.DMA((2,)),
                        pltpu.VMEM((tmx, d), BF16)],
    )
    return pl.pallas_call(
        kern,
        grid_spec=grid_spec,
        out_shape=jax.ShapeDtypeStruct((n_tiles * tmx, d), F32),
        compiler_params=_params(("arbitrary", "arbitrary"), vmem),
        name="moe_ffn",
    )(tile_expert, slot_token, n_active, x, gain, slot_gate, wg, wu, wd)


def _moe_combine_kernel(slot_ref, x_ref, ys_hbm, o_ref, buf_ref, sem_ref, *, tmc):
    t = pl.program_id(0)
    cur = t % 2

    def start_rows(tile, b):
        def body(i, carry):
            for k in range(2):
                s = slot_ref[2 * (tile * tmc + i) + k]
                pltpu.make_async_copy(ys_hbm.at[pl.ds(s, 1)], buf_ref.at[b, k, pl.ds(i, 1)],
                                      sem_ref.at[b]).start()
            return carry
        lax.fori_loop(0, tmc, body, 0)

    @pl.when(t == 0)
    def _():
        start_rows(0, 0)

    @pl.when(t + 1 < pl.num_programs(0))
    def _():
        start_rows(t + 1, 1 - cur)

    pltpu.make_async_copy(buf_ref.at[cur], buf_ref.at[cur], sem_ref.at[cur]).wait()
    o_ref[...] = x_ref[...] + buf_ref[cur, 0] + buf_ref[cur, 1]


def _moe_combine(dm, x, ys, slot):
    d = dm.d
    tmc = _largest_divisor(dm.m, 384, SUBLANES)
    kern = functools.partial(_moe_combine_kernel, tmc=tmc)
    vmem = 4 * tmc * d * 4 + 4 * tmc * d * 4 + (2 << 20)
    grid_spec = pltpu.PrefetchScalarGridSpec(
        num_scalar_prefetch=1,
        grid=(dm.m // tmc,),
        in_specs=[
            pl.BlockSpec((tmc, d), lambda t, s: (t, 0)),
            pl.BlockSpec(memory_space=pl.ANY),
        ],
        out_specs=pl.BlockSpec((tmc, d), lambda t, s: (t, 0)),
        scratch_shapes=[pltpu.VMEM((2, 2, tmc, d), F32),
                        pltpu.SemaphoreType.DMA((2,))],
    )
    return pl.pallas_call(
        kern,
        grid_spec=grid_spec,
        out_shape=jax.ShapeDtypeStruct((dm.m, d), F32),
        compiler_params=_params(("arbitrary",), vmem),
        name="moe_combine",
    )(slot, x, ys)


def _hi_lo(w):
    hi = w.astype(BF16)
    return hi, (w - hi.astype(F32)).astype(BF16)


def _pad_lanes(a):
    return jnp.pad(a, [(0, 0)] * (a.ndim - 1) + [(0, LANES - a.shape[-1])])


def _make_dims(x_prompt, x_sample, cache_k, page_table, state_conv, state_ssm, q_norm):
    bp, lp, d = x_prompt.shape
    bs, ls, _ = x_sample.shape
    _, _, page, heads, hd = cache_k.shape
    dk = q_norm.shape[-1]
    assert hd == 2 * dk == LANES
    aw = heads * hd
    cw = state_conv.shape[2] + 1
    cd = state_conv.shape[3]
    _, _, sh, sp, ns = state_ssm.shape
    di = sh * sp
    sg = (cd - di) // (2 * ns)
    assert aw + di == d and aw == di and 2 * sp == LANES and ns == LANES and (sh // sg) % 2 == 0
    in_main = 3 * aw + di + cd
    n_pages = page_table.shape[1]
    mp, ms = bp * lp, bs * ls
    m = mp + ms
    assert ls == SUBLANES and mp % ls == 0
    return Dims(
        d=d, bp=bp, lp=lp, bs=bs, ls=ls, mp=mp, ms=ms, m=m, heads=heads, dk=dk, aw=aw, di=di,
        sh=sh, sp=sp, sg=sg, ns=ns, cw=cw, cd=cd, in_main=in_main, past=n_pages * page,
        page=page, n_pages=n_pages,
        tm=_largest_divisor(m, 768, BF16_ROWS),
        tn=_largest_divisor(math.gcd(math.gcd(2 * aw, in_main), d), 512, LANES),
        tq=_largest_divisor(lp, 512, LANES),
        lc=128 if lp % 128 == 0 else lp,
        pps=_largest_divisor(n_pages, 8, 1),
        tf=256,
        tmx=512 if m >= 4096 else 128)


def kernel(x_prompt, x_sample, cache_k, cache_v, page_table, state_conv, state_ssm, norm_mix, w_in, q_norm, k_norm, lambda_q1, lambda_k1, lambda_q2, lambda_k2, attn_subln, conv_w, conv_b, dt_bias, a_log, d_skip, ssm_norm, w_out, norm_ffn, dense_w_gate, dense_w_up, dense_w_down, router_w, expert_w_gate, expert_w_up, expert_w_down):
    dm = _make_dims(x_prompt, x_sample, cache_k, page_table, state_conv, state_ssm, q_norm)
    depth = w_in.shape[0]
    n_experts = router_w.shape[-1]
    hd = 2 * dm.dk
    npairs = dm.sh // 2

    x = jnp.concatenate([x_prompt.reshape(dm.mp, dm.d), x_sample.reshape(dm.ms, dm.d)], axis=0)
    pt = page_table.reshape(-1).astype(jnp.int32)
    gi = lax.broadcasted_iota(jnp.int32, (LANES, LANES), 0) // dm.dk
    gj = lax.broadcasted_iota(jnp.int32, (LANES, LANES), 1) // dm.dk
    bd = (gi == gj).astype(BF16)
    zero_conv = jnp.zeros((dm.bp, dm.cw - 1, dm.cd), F32)
    zero_h = jnp.zeros((dm.bp, npairs, 2 * dm.sp, dm.ns), F32)

    outs = [[] for _ in range(8)]
    for l in range(depth):
        lam_init = 0.8 - 0.6 * math.exp(-0.3 * l)
        lam = (jnp.exp(jnp.sum(lambda_q1[l] * lambda_k1[l]))
               - jnp.exp(jnp.sum(lambda_q2[l] * lambda_k2[l])) + lam_init).reshape(1).astype(F32)
        reps = 2 * dm.aw // dm.dk // 2
        qkg = jnp.concatenate([jnp.tile(q_norm[l] * (dm.dk ** -0.5), reps),
                               jnp.tile(k_norm[l], reps)]).reshape(1, 2 * dm.aw)
        w_main = w_in[l][:, :dm.in_main].astype(BF16)
        wdt_hi, wdt_lo = _hi_lo(_pad_lanes(w_in[l][:, dm.in_main:]))
        u, dtraw = _inproj(dm, x, norm_mix[l].reshape(1, dm.d), w_main, qkg, bd, wdt_hi, wdt_lo)

        subln = attn_subln[l].reshape(1, hd)
        o_p = _attn_prompt(dm, 1.0 - lam_init, lam, u, subln)
        o_s = _attn_sample(dm, l, 1.0 - lam_init, pt, lam, u, cache_k, cache_v, subln)

        ssd_args = (conv_w[l], conv_b[l].reshape(1, dm.cd), _pad_lanes(dt_bias[l].reshape(1, dm.sh)),
                    _pad_lanes(a_log[l].reshape(1, dm.sh)), _pad_lanes(d_skip[l].reshape(1, dm.sh)),
                    ssm_norm[l].reshape(1, dm.di))
        y_p, conv_p, h_p = _ssd(dm, u, dtraw, *ssd_args, zero_conv, zero_h,
                                nb=dm.bp, lseq=dm.lp, lc=dm.lc, row0=0)
        y_s, conv_s, h_s = _ssd(dm, u, dtraw, *ssd_args, state_conv[l],
                                state_ssm[l].reshape(dm.bs, npairs, 2 * dm.sp, dm.ns),
                                nb=dm.bs, lseq=dm.ls, lc=dm.ls, row0=dm.mp // dm.ls)

        x = _outproj(dm, o_p, y_p, o_s, y_s, w_out[l].astype(BF16), x)

        j = l // 2
        gain = norm_ffn[l].reshape(1, dm.d)
        if l % 2 == 0:
            x = _ffn(dm, x, gain, dense_w_gate[j].astype(BF16), dense_w_up[j].astype(BF16),
                     dense_w_down[j].astype(BF16))
        else:
            r_hi, r_lo = _hi_lo(_pad_lanes(router_w[j]))
            route = _router(dm, x, gain, r_hi, r_lo, n_experts)
            tmx = dm.tmx
            n_tiles = -(-(2 * dm.m + n_experts * (tmx - 1)) // tmx)
            plan = _moe_plan(route, n_experts, tmx, n_tiles)
            ys = _moe_ffn(dm, x, gain, expert_w_gate[j].astype(BF16), expert_w_up[j].astype(BF16),
                          expert_w_down[j].astype(BF16), plan, tmx, n_tiles)
            x = _moe_combine(dm, x, ys, plan[4])

        kk = u[:, dm.aw:2 * dm.aw]
        vv = u[:, 2 * dm.aw:3 * dm.aw]
        outs[0].append(kk[:dm.mp].reshape(dm.bp, dm.lp, dm.heads, hd))
        outs[1].append(vv[:dm.mp].reshape(dm.bp, dm.lp, dm.heads, hd))
        outs[2].append(conv_p)
        outs[3].append(h_p.reshape(dm.bp, dm.sh, dm.sp, dm.ns))
        outs[4].append(kk[dm.mp:].reshape(dm.bs, dm.ls, dm.heads, hd))
        outs[5].append(vv[dm.mp:].reshape(dm.bs, dm.ls, dm.heads, hd))
        outs[6].append(conv_s)
        outs[7].append(h_s.reshape(dm.bs, dm.sh, dm.sp, dm.ns))

    y_prompt = x[:dm.mp].reshape(dm.bp, dm.lp, dm.d)
    y_sample = x[dm.mp:].reshape(dm.bs, dm.ls, dm.d)
    return (y_prompt, y_sample) + tuple(jnp.stack(o) for o in outs)
```
